```python
import math
import jax, jax.numpy as jnp
from jax import lax
import numpy as np

D_MODEL = 1024
BATCH = 8
SEQ = 8192
DEPTH = 4
DEC_BATCH = 8
DEC_SEQ = 64
PAST_LEN = 1024

CHUNK = 64
N_META = 16
N_HEADS = 8
HEAD_DIM = 64
HD2 = 2 * HEAD_DIM
ATTN_WIDTH = N_HEADS * HD2
N_BUCKETS = 32
MAX_DISTANCE = 128
POOL_WINDOWS = (2, 4, 8, 16)
N_POOL_GROUPS = len(POOL_WINDOWS)
POOL_GROUP = D_MODEL // N_POOL_GROUPS
POOL_STATE = max(POOL_WINDOWS) - 1
D_FF = 4 * D_MODEL
N_ATTN_LAYERS = (DEPTH + 1) // 2
N_POOL_LAYERS = DEPTH // 2
Q_BLOCK = 128
EPS = 1e-6
SUBLN_EPS = 1e-5
NEG_INF = -1e30

kernel_name = "hybrid_diffattn_pool_streaming_step"


def rms_norm(x, g, eps=EPS):
    xf = x.astype(jnp.float32)
    y = xf * lax.rsqrt(jnp.mean(xf * xf, axis=-1, keepdims=True) + eps)
    return (y * g.astype(jnp.float32)).astype(x.dtype)


def t5_bucket(rel):
    nb = N_BUCKETS // 2
    max_exact = nb // 2
    ret = jnp.where(rel > 0, nb, 0)
    n = jnp.abs(rel)
    nf = jnp.maximum(n, 1).astype(jnp.float32)
    large = max_exact + (jnp.log(nf / max_exact) / math.log(MAX_DISTANCE / max_exact)
                         * (nb - max_exact)).astype(jnp.int32)
    large = jnp.minimum(large, nb - 1)
    return ret + jnp.where(n < max_exact, n, large)


def rel_bias(q_pos, k_pos, table):
    bucket = t5_bucket(k_pos[None, :] - q_pos[:, None])
    return jnp.transpose(table[bucket].astype(jnp.float32), (2, 0, 1))


def diff_lambda(lq1, lk1, lq2, lk2, lam_init):
    f32 = lambda a: a.astype(jnp.float32)
    return (jnp.exp(jnp.sum(f32(lq1) * f32(lk1))) - jnp.exp(jnp.sum(f32(lq2) * f32(lk2)))
            + lam_init)


def qkv_proj(h, w_qkv):
    B, L, _ = h.shape
    qkv = jnp.einsum('bld,de->ble', h, w_qkv).reshape(B, L, 3, N_HEADS, HD2)
    return qkv[:, :, 0], qkv[:, :, 1], qkv[:, :, 2]


def diff_attend(q, k, v, bias, mask, lam, subln_g, lam_init):
    scale = HEAD_DIM ** -0.5
    s1 = jnp.einsum('bqhd,bkhd->bhqk', q[..., :HEAD_DIM], k[..., :HEAD_DIM]).astype(jnp.float32) * scale + bias
    s2 = jnp.einsum('bqhd,bkhd->bhqk', q[..., HEAD_DIM:], k[..., HEAD_DIM:]).astype(jnp.float32) * scale + bias
    s1 = jnp.where(mask, s1, NEG_INF)
    s2 = jnp.where(mask, s2, NEG_INF)
    p = jax.nn.softmax(s1, axis=-1) - lam * jax.nn.softmax(s2, axis=-1)
    o = jnp.einsum('bhqk,bkhd->bqhd', p.astype(v.dtype), v)
    o = rms_norm(o, subln_g, SUBLN_EPS) * (1.0 - lam_init)
    B, Lq = q.shape[:2]
    return o.reshape(B, Lq, ATTN_WIDTH)


def attend_rows(q, k_new, v_new, prev_k, prev_v, prev_pos, pos0, lam, subln_g, lam_init, rel_table):
    B, L = q.shape[:2]
    P = prev_k.shape[1]
    k_all = jnp.concatenate([jnp.broadcast_to(prev_k, (B,) + prev_k.shape[1:]), k_new], axis=1)
    v_all = jnp.concatenate([jnp.broadcast_to(prev_v, (B,) + prev_v.shape[1:]), v_new], axis=1)
    k_pos = jnp.concatenate([prev_pos.astype(jnp.int32), pos0 + jnp.arange(L, dtype=jnp.int32)])
    k_chunk = jnp.concatenate([jnp.full((P,), -1, jnp.int32), jnp.arange(L, dtype=jnp.int32) // CHUNK])

    def block(q_blk, start):
        idx = start + jnp.arange(q_blk.shape[1], dtype=jnp.int32)
        mask = k_chunk[None, :] <= (idx // CHUNK)[:, None]
        bias = rel_bias(pos0 + idx, k_pos, rel_table)
        return diff_attend(q_blk, k_all, v_all, bias, mask, lam, subln_g, lam_init)

    if L > Q_BLOCK:
        nblk = L // Q_BLOCK
        qb = q.reshape(B, nblk, Q_BLOCK, N_HEADS, HD2).swapaxes(0, 1)
        o = lax.map(lambda a: block(a[0], a[1] * Q_BLOCK), (qb, jnp.arange(nblk, dtype=jnp.int32)))
        return o.swapaxes(0, 1).reshape(B, L, ATTN_WIDTH)
    return block(q, 0)


def pool_mix(ext, n_prev, w_pool, scale):
    B, T, _ = ext.shape
    h = ext[:, n_prev:]
    L = T - n_prev
    c = jnp.cumsum(ext.astype(jnp.float32), axis=1)
    cnt = jnp.arange(1, T + 1, dtype=jnp.float32)[None, :, None]
    means = []
    for g, w in enumerate(POOL_WINDOWS):
        cg = c[..., g * POOL_GROUP:(g + 1) * POOL_GROUP]
        lower = jnp.pad(cg, ((0, 0), (w, 0), (0, 0)))[:, :T]
        means.append((cg - lower) / jnp.minimum(cnt, w))
    pooled = jnp.concatenate(means, axis=-1)[:, n_prev:]
    d = (pooled - h.astype(jnp.float32)).astype(ext.dtype).reshape(B, L, N_POOL_GROUPS, POOL_GROUP)
    y = jnp.einsum('blgc,gce->blge', d, w_pool).reshape(B, L, D_MODEL)
    return y * scale


def sq_relu_mlp(h, w_up, w_down):
    u = jnp.einsum('bld,df->blf', h, w_up)
    return jnp.einsum('blf,fd->bld', jnp.square(jax.nn.relu(u)), w_down)


def trunk(x, prev_k, prev_v, prev_pos, pos0, pool_prev, state_only_last,
          rel_table, norm_mix_pre, norm_mix_post, norm_ffn_pre, norm_ffn_post,
          w_qkv, lambda_q1, lambda_k1, lambda_q2, lambda_k2, subln_g, w_o,
          w_pool, pool_scale, w_up, w_down):
    ks, vs, tails = [], [], []
    for i in range(DEPTH):
        h = rms_norm(x, norm_mix_pre[i])
        last_state_only = state_only_last and i == DEPTH - 1
        if i % 2 == 0:
            a = i // 2
            q, k, v = qkv_proj(h, w_qkv[a])
            ks.append(k)
            vs.append(v)
            if last_state_only:
                break
            lam_init = 0.8 - 0.6 * math.exp(-0.3 * i)
            lam = diff_lambda(lambda_q1[a], lambda_k1[a], lambda_q2[a], lambda_k2[a], lam_init)
            o = attend_rows(q, k, v, prev_k[a], prev_v[a], prev_pos, pos0, lam, subln_g[a], lam_init, rel_table)
            y = jnp.einsum('ble,ed->bld', o, w_o[a])
        else:
            p = i // 2
            prev = pool_prev[p]
            ext = jnp.concatenate([jnp.broadcast_to(prev, (h.shape[0],) + prev.shape[1:]), h], axis=1)
            tails.append(ext[:, -POOL_STATE:])
            if last_state_only:
                break
            y = pool_mix(ext, prev.shape[1], w_pool[p], pool_scale[p])
        x = x + rms_norm(y, norm_mix_post[i])
        x = x + rms_norm(sq_relu_mlp(rms_norm(x, norm_ffn_pre[i]), w_up[i], w_down[i]), norm_ffn_post[i])
    return x, ks, vs, tails


def setup_inputs(seed: int = 0) -> dict:
    key = jax.random.key(seed)
    ks = jax.random.split(key, 24)
    n = lambda k, s, sc: jax.random.normal(k, s, jnp.float32) * sc
    return {
        "x_prompt": n(ks[0], (BATCH, SEQ, D_MODEL), 1.0),
        "x_sample": n(ks[1], (DEC_BATCH, DEC_SEQ, D_MODEL), 1.0),
        "cache_k": n(ks[2], (N_ATTN_LAYERS, DEC_BATCH, PAST_LEN, N_HEADS, HD2), 1.0),
        "cache_v": n(ks[3], (N_ATTN_LAYERS, DEC_BATCH, PAST_LEN, N_HEADS, HD2), 1.0),
        "state_pool": n(ks[4], (N_POOL_LAYERS, DEC_BATCH, POOL_STATE, D_MODEL), 1.0),
        "meta_tokens": n(ks[5], (N_META, D_MODEL), 1.0),
        "rel_bias_table": n(ks[6], (N_BUCKETS, N_HEADS), 0.5),
        "norm_mix_pre": 1.0 + n(ks[7], (DEPTH, D_MODEL), 0.05),
        "norm_mix_post": 1.0 + n(ks[8], (DEPTH, D_MODEL), 0.05),
        "norm_ffn_pre": 1.0 + n(ks[9], (DEPTH, D_MODEL), 0.05),
        "norm_ffn_post": 1.0 + n(ks[10], (DEPTH, D_MODEL), 0.05),
        "w_qkv": n(ks[11], (N_ATTN_LAYERS, D_MODEL, 3 * ATTN_WIDTH), D_MODEL ** -0.5),
        "lambda_q1": n(ks[12], (N_ATTN_LAYERS, HEAD_DIM), 0.1),
        "lambda_k1": n(ks[13], (N_ATTN_LAYERS, HEAD_DIM), 0.1),
        "lambda_q2": n(ks[14], (N_ATTN_LAYERS, HEAD_DIM), 0.1),
        "lambda_k2": n(ks[15], (N_ATTN_LAYERS, HEAD_DIM), 0.1),
        "subln_g": 1.0 + n(ks[16], (N_ATTN_LAYERS, HD2), 0.05),
        "w_o": n(ks[17], (N_ATTN_LAYERS, ATTN_WIDTH, D_MODEL), ATTN_WIDTH ** -0.5),
        "w_pool": n(ks[18], (N_POOL_LAYERS, N_POOL_GROUPS, POOL_GROUP, POOL_GROUP), POOL_GROUP ** -0.5),
        "pool_scale": 1.0 + n(ks[19], (N_POOL_LAYERS, D_MODEL), 0.1),
        "w_up": n(ks[20], (DEPTH, D_MODEL, D_FF), D_MODEL ** -0.5),
        "w_down": n(ks[21], (DEPTH, D_FF, D_MODEL), D_FF ** -0.5),
    }


def reference(x_prompt, x_sample, cache_k, cache_v, state_pool, meta_tokens, rel_bias_table,
              norm_mix_pre, norm_mix_post, norm_ffn_pre, norm_ffn_post, w_qkv,
              lambda_q1, lambda_k1, lambda_q2, lambda_k2, subln_g, w_o, w_pool, pool_scale,
              w_up, w_down):
    weights = (rel_bias_table, norm_mix_pre, norm_mix_post, norm_ffn_pre, norm_ffn_post,
               w_qkv, lambda_q1, lambda_k1, lambda_q2, lambda_k2, subln_g, w_o,
               w_pool, pool_scale, w_up, w_down)
    mdt = meta_tokens.dtype

    empty_kv = [jnp.zeros((1, 0, N_HEADS, HD2), mdt) for _ in range(N_ATTN_LAYERS)]
    empty_pool = [jnp.zeros((1, 0, D_MODEL), mdt) for _ in range(N_POOL_LAYERS)]
    _, meta_k, meta_v, meta_tail = trunk(meta_tokens[None], empty_kv, empty_kv,
                                         jnp.zeros((0,), jnp.int32), 0, empty_pool, True, *weights)

    y_prompt, k_p, v_p, tail_p = trunk(x_prompt, meta_k, meta_v, jnp.arange(N_META, dtype=jnp.int32),
                                       N_META, meta_tail, False, *weights)

    past = cache_k.shape[2]
    bd = x_sample.shape[0]
    prev_k_s = [jnp.concatenate([jnp.broadcast_to(meta_k[a], (bd,) + meta_k[a].shape[1:]), cache_k[a]], axis=1)
                for a in range(N_ATTN_LAYERS)]
    prev_v_s = [jnp.concatenate([jnp.broadcast_to(meta_v[a], (bd,) + meta_v[a].shape[1:]), cache_v[a]], axis=1)
                for a in range(N_ATTN_LAYERS)]
    prev_pos_s = jnp.concatenate([jnp.arange(N_META, dtype=jnp.int32) - N_META,
                                  jnp.arange(past, dtype=jnp.int32)])
    y_sample, k_s, v_s, tail_s = trunk(x_sample, prev_k_s, prev_v_s, prev_pos_s, past,
                                       [state_pool[p] for p in range(N_POOL_LAYERS)], False, *weights)

    bp = x_prompt.shape[0]
    k_prompt = jnp.stack([jnp.concatenate([jnp.broadcast_to(meta_k[a], (bp,) + meta_k[a].shape[1:]), k_p[a]], axis=1)
                          for a in range(N_ATTN_LAYERS)])
    v_prompt = jnp.stack([jnp.concatenate([jnp.broadcast_to(meta_v[a], (bp,) + meta_v[a].shape[1:]), v_p[a]], axis=1)
                          for a in range(N_ATTN_LAYERS)])
    pool_prompt = jnp.stack(tail_p)
    k_sample = jnp.stack(k_s)
    v_sample = jnp.stack(v_s)
    pool_sample = jnp.stack(tail_s)
    return (y_prompt, y_sample, k_prompt, v_prompt, pool_prompt, k_sample, v_sample, pool_sample)
```

```python
import functools
import math

import jax
import jax.numpy as jnp
import numpy as np
from jax import lax
from jax.experimental import pallas as pl
from jax.experimental.pallas import tpu as pltpu

CHUNK = 64
N_HEADS = 8
HEAD_DIM = 64
HD2 = 2 * HEAD_DIM
N_BUCKETS = 32
MAX_DISTANCE = 128
POOL_WINDOWS = (2, 4, 8, 16)
POOL_STATE = max(POOL_WINDOWS) - 1
EPS = 1e-6
SUBLN_EPS = 1e-5
NEG_INF = -1e30
FAR_BUCKET = N_BUCKETS // 2 - 1
MASKED = N_BUCKETS

TOKEN_TILE = 512
ATTN_BLOCK = 512
ATTN_ROWS = 256
FF_CHUNK = 1024
HALO = 16
LANES = 128

BF16 = jnp.bfloat16
F32 = jnp.float32


def _rms(x, g, eps):
    return x * lax.rsqrt(jnp.mean(x * x, axis=-1, keepdims=True) + eps) * g


def _dot(a, b):
    return jnp.dot(a, b, preferred_element_type=F32)


def _dot_nt(a, b):
    return lax.dot_general(a, b, (((1,), (1,)), ((), ())), preferred_element_type=F32)


def _resident(shape):
    nd = len(shape)
    return pl.BlockSpec(shape, lambda *_: (0,) * nd, pipeline_mode=pl.Buffered(1))


def _qkv_kernel(x_ref, g_ref, w_ref, q_ref, k_ref, v_ref, kb_ref, vb_ref):
    width = q_ref.shape[1]
    h = _rms(x_ref[...], g_ref[...], EPS).astype(BF16)
    q = _dot(h, w_ref[:, 0:width])
    q_ref[...] = (q * (HEAD_DIM ** -0.5)).astype(BF16)
    k = _dot(h, w_ref[:, width:2 * width])
    k_ref[...] = k
    kb_ref[...] = k.astype(BF16)
    v = _dot(h, w_ref[:, 2 * width:3 * width])
    v_ref[...] = v
    vb_ref[...] = v.astype(BF16)


def _qkv(x, g, w):
    n, d = x.shape
    width = w.shape[1] // 3
    tm = min(TOKEN_TILE, n)
    row = lambda c: pl.BlockSpec((tm, c), lambda i: (i, 0))
    return pl.pallas_call(
        _qkv_kernel,
        grid=(n // tm,),
        in_specs=[row(d), _resident((1, d)), _resident(w.shape)],
        out_specs=[row(width)] * 5,
        out_shape=[jax.ShapeDtypeStruct((n, width), BF16),
                   jax.ShapeDtypeStruct((n, width), F32),
                   jax.ShapeDtypeStruct((n, width), F32),
                   jax.ShapeDtypeStruct((n, width), BF16),
                   jax.ShapeDtypeStruct((n, width), BF16)],
        compiler_params=pltpu.CompilerParams(dimension_semantics=("parallel",)),
        name="qkv_proj",
    )(x, g, w)


def _mlp_kernel(*refs, has_proj):
    if has_proj:
        x_ref, o_ref, wo_ref, gmix_ref, gpre_ref, wup_ref, wdn_ref, gpost_ref, out_ref = refs
    else:
        x_ref, gpre_ref, wup_ref, wdn_ref, gpost_ref, out_ref = refs
    x = x_ref[...]
    if has_proj:
        x = x + _rms(_dot(o_ref[...], wo_ref[...]), gmix_ref[...], EPS)
    h = _rms(x, gpre_ref[...], EPS).astype(BF16)
    d_ff = wup_ref.shape[1]
    acc = jnp.zeros(x.shape, F32)
    for c in range(d_ff // FF_CHUNK):
        cols = slice(c * FF_CHUNK, (c + 1) * FF_CHUNK)
        u = jnp.maximum(_dot(h, wup_ref[:, cols]), 0.0)
        acc = acc + _dot((u * u).astype(BF16), wdn_ref[cols, :])
    out_ref[...] = x + _rms(acc, gpost_ref[...], EPS)


def _mlp(x, g_pre, w_up, w_down, g_post, proj=None):
    n, d = x.shape
    tm = min(TOKEN_TILE, n)
    row = lambda c: pl.BlockSpec((tm, c), lambda i: (i, 0))
    args, specs = [x], [row(d)]
    if proj is not None:
        o, w_o, g_mix = proj
        args += [o, w_o, g_mix]
        specs += [row(o.shape[1]), _resident(w_o.shape), _resident((1, d))]
    args += [g_pre, w_up, w_down, g_post]
    specs += [_resident((1, d)), _resident(w_up.shape), _resident(w_down.shape), _resident((1, d))]
    return pl.pallas_call(
        functools.partial(_mlp_kernel, has_proj=proj is not None),
        grid=(n // tm,),
        in_specs=specs,
        out_specs=row(d),
        out_shape=jax.ShapeDtypeStruct((n, d), F32),
        compiler_params=pltpu.CompilerParams(dimension_semantics=("parallel",)),
        name="proj_mlp" if proj is not None else "mlp",
    )(*args)


def _pool_kernel(x_ref, xh_ref, prev_ref, gpre_ref, w_ref, scale_ref, gpost_ref, out_ref, tail_ref,
                 a_ref, b_ref, c_ref, d_ref, *, n_prev, tiles_per_seq):
    tm, d = x_ref.shape
    group = d // len(POOL_WINDOWS)
    pad = 8
    i = pl.program_id(0)
    t = i % tiles_per_seq
    x = x_ref[...]
    g = gpre_ref[...]
    h = _rms(x, g, EPS)
    halo = jnp.where(t == 0, prev_ref[0], _rms(xh_ref[...], g, EPS))
    for ref in (a_ref, b_ref, c_ref, d_ref):
        ref[0:pad, :] = jnp.zeros((pad, d), F32)
    a_ref[pad:pad + HALO, :] = halo
    a_ref[pad + HALO:pad + HALO + tm, :] = h
    rows = HALO + tm
    b_ref[pad:pad + rows, :] = a_ref[pad:pad + rows, :] + a_ref[pad - 1:pad - 1 + rows, :]
    c_ref[pad:pad + rows, group:] = b_ref[pad:pad + rows, group:] + b_ref[pad - 2:pad - 2 + rows, group:]
    d_ref[pad:pad + rows, 2 * group:] = (c_ref[pad:pad + rows, 2 * group:]
                                         + c_ref[pad - 4:pad - 4 + rows, 2 * group:])
    new = slice(pad + HALO, pad + HALO + tm)
    sums = (b_ref[new, 0:group],
            c_ref[new, group:2 * group],
            d_ref[new, 2 * group:3 * group],
            d_ref[new, 3 * group:] + d_ref[pad + HALO - 8:pad + HALO - 8 + tm, 3 * group:])
    cnt = (n_prev + 1 + t * tm + lax.broadcasted_iota(jnp.int32, (tm, 1), 0)).astype(F32)
    ys = []
    for gi, w in enumerate(POOL_WINDOWS):
        mean = sums[gi] / jnp.minimum(cnt, float(w))
        diff = (mean - h[:, gi * group:(gi + 1) * group]).astype(BF16)
        ys.append(_dot(diff, w_ref[gi]))
    y = jnp.concatenate(ys, axis=1) * scale_ref[...]
    out_ref[...] = x + _rms(y, gpost_ref[...], EPS)
    tail_ref[0] = h[tm - HALO:, :]


def _pool(x, prev, seq_len, n_prev, g_pre, w_pool, scale, g_post):
    n, d = x.shape
    tm = min(TOKEN_TILE, seq_len)
    tiles_per_seq = seq_len // tm
    n_seq = n // seq_len
    shared_prev = prev.shape[0] == 1
    halo_blocks = tm // HALO
    scratch = [pltpu.VMEM((8 + HALO + tm, d), F32)] * 4
    return pl.pallas_call(
        functools.partial(_pool_kernel, n_prev=n_prev, tiles_per_seq=tiles_per_seq),
        grid=(n // tm,),
        in_specs=[pl.BlockSpec((tm, d), lambda i: (i, 0)),
                  pl.BlockSpec((HALO, d), lambda i: (jnp.maximum(i * halo_blocks - 1, 0), 0)),
                  pl.BlockSpec((1, HALO, d), (lambda i: (0, 0, 0)) if shared_prev
                               else (lambda i: (i // tiles_per_seq, 0, 0))),
                  _resident((1, d)), _resident(w_pool.shape), _resident((1, d)), _resident((1, d))],
        out_specs=[pl.BlockSpec((tm, d), lambda i: (i, 0)),
                   pl.BlockSpec((1, HALO, d), lambda i: (i // tiles_per_seq, 0, 0))],
        out_shape=[jax.ShapeDtypeStruct((n, d), F32), jax.ShapeDtypeStruct((n_seq, HALO, d), F32)],
        scratch_shapes=scratch,
        compiler_params=pltpu.CompilerParams(dimension_semantics=("arbitrary",)),
        name="pool_mix",
    )(x, x, prev, g_pre, w_pool, scale, g_post)


def _bucket_np(rel):
    nb = N_BUCKETS // 2
    max_exact = nb // 2
    n = np.abs(rel)
    ratio = np.log(np.maximum(n, 1) / max_exact) / math.log(MAX_DISTANCE / max_exact) * (nb - max_exact)
    large = np.minimum(max_exact + np.floor(ratio + 1e-6).astype(np.int64), nb - 1)
    return (np.where(rel > 0, nb, 0) + np.where(n < max_exact, n, large)).astype(np.int32)


def _bias_kernel(tab_ref, idx_ref, out_ref):
    h = pl.program_id(0)
    idx = idx_ref[...]
    base = tab_ref[FAR_BUCKET, h]
    out = jnp.full(idx.shape, NEG_INF, F32)
    for b in range(N_BUCKETS):
        out = jnp.where(idx == b, tab_ref[b, h] - base, out)
    out_ref[0] = out


def _bias_tiles(table, idx):
    r, c = idx.shape
    tr = min(r, 256)
    n_heads = table.shape[1]
    return pl.pallas_call(
        _bias_kernel,
        grid=(n_heads, r // tr),
        in_specs=[pl.BlockSpec(memory_space=pltpu.SMEM), pl.BlockSpec((tr, c), lambda h, i: (i, 0))],
        out_specs=pl.BlockSpec((1, tr, c), lambda h, i: (h, i, 0)),
        out_shape=jax.ShapeDtypeStruct((n_heads, r, c), F32),
        compiler_params=pltpu.CompilerParams(dimension_semantics=("parallel", "parallel")),
        name="rel_bias_tiles",
    )(table, jnp.asarray(idx))


def _diff_lambda(lq1, lk1, lq2, lk2, lam_init):
    return (jnp.exp(jnp.sum(lq1[...] * lk1[...], keepdims=True))
            - jnp.exp(jnp.sum(lq2[...] * lk2[...], keepdims=True)) + lam_init)


def _split_maps(q):
    lane = lax.broadcasted_iota(jnp.int32, q.shape, 1)
    zero = jnp.zeros_like(q)
    return jnp.where(lane < HEAD_DIM, q, zero), jnp.where(lane >= HEAD_DIM, q, zero)


def _subln(o, g, lam_init):
    return _rms(o, g, SUBLN_EPS) * (1.0 - lam_init)


def _small_attn_kernel(q_ref, k_ref, v_ref, bias_ref, lq1, lk1, lq2, lk2, g_ref, o_ref, *, lam_init):
    q1, q2 = _split_maps(q_ref[...])
    k = k_ref[0]
    bias = bias_ref[0]

    def softmax(s):
        e = jnp.exp(s - s.max(axis=1, keepdims=True))
        return e / e.sum(axis=1, keepdims=True)

    lam = _diff_lambda(lq1, lk1, lq2, lk2, lam_init)
    p = softmax(_dot_nt(q1, k) + bias) - lam * softmax(_dot_nt(q2, k) + bias)
    o = _dot(p.astype(BF16), v_ref[0])
    o_ref[...] = _subln(o, g_ref[...], lam_init).astype(BF16)


def _small_attn(q, k_all, v_all, bias, lams, g, lam_init):
    n_seq, lk, width = k_all.shape
    lq = q.shape[0] // n_seq
    n_heads = width // HD2
    vec = lambda c: pl.BlockSpec((1, c), lambda b, h: (0, 0))
    return pl.pallas_call(
        functools.partial(_small_attn_kernel, lam_init=lam_init),
        grid=(n_seq, n_heads),
        in_specs=[pl.BlockSpec((lq, HD2), lambda b, h: (b, h)),
                  pl.BlockSpec((1, lk, HD2), lambda b, h: (b, 0, h)),
                  pl.BlockSpec((1, lk, HD2), lambda b, h: (b, 0, h)),
                  pl.BlockSpec((1, lq, lk), lambda b, h: (h, 0, 0)),
                  vec(HEAD_DIM), vec(HEAD_DIM), vec(HEAD_DIM), vec(HEAD_DIM), vec(HD2)],
        out_specs=pl.BlockSpec((lq, HD2), lambda b, h: (b, h)),
        out_shape=jax.ShapeDtypeStruct(q.shape, BF16),
        compiler_params=pltpu.CompilerParams(dimension_semantics=("parallel", "parallel")),
        name="short_attention",
    )(q, k_all, v_all, bias, *lams, g)


def _long_attn_kernel(q_ref, k_ref, v_ref, km_ref, vm_ref, bd_ref, bp_ref, bm_ref, lq1, lk1, lq2, lk2, g_ref,
                      o_ref, qs_ref, m_ref, l_ref, acc_ref, *, lam_init):
    tq = q_ref.shape[0]
    i = pl.program_id(2)
    q1, q2 = _split_maps(q_ref[...])
    qs_ref[0:tq, :] = q1
    qs_ref[tq:2 * tq, :] = q2

    def update(kblk, vblk, bias_rows, first):
        for c in range(2 * tq // ATTN_ROWS):
            rows = slice(c * ATTN_ROWS, (c + 1) * ATTN_ROWS)
            s = _dot_nt(qs_ref[rows, :], kblk)
            if bias_rows is not None:
                s = s + bias_rows((c * ATTN_ROWS) % tq)
            m_cur = s.max(axis=1, keepdims=True)
            if first:
                p = jnp.exp(s - m_cur)
                m_ref[rows, :] = m_cur
                l_ref[rows, :] = p.sum(axis=1, keepdims=True)
                acc_ref[rows, :] = _dot(p.astype(BF16), vblk)
            else:
                m_prev = m_ref[rows, :]
                m_new = jnp.maximum(m_prev, m_cur)
                alpha = jnp.exp(m_prev - m_new)
                p = jnp.exp(s - m_new)
                m_ref[rows, :] = m_new
                l_ref[rows, :] = alpha * l_ref[rows, :] + p.sum(axis=1, keepdims=True)
                acc_ref[rows, :] = alpha * acc_ref[rows, :] + _dot(p.astype(BF16), vblk)

    def key_block(j):
        off = pl.multiple_of(j * tq, tq)
        return k_ref[pl.ds(off, tq), :], v_ref[pl.ds(off, tq), :]

    update(km_ref[...], vm_ref[...], lambda r0: bm_ref[0, 0, r0:r0 + ATTN_ROWS, :], True)

    def far_body(j, carry):
        update(*key_block(j), None, False)
        return carry

    lax.fori_loop(0, jnp.maximum(i - 1, 0), far_body, 0)

    @pl.when(i >= 1)
    def _():
        update(*key_block(i - 1), lambda r0: bp_ref[0, r0:r0 + ATTN_ROWS, :], False)

    update(*key_block(i), lambda r0: bd_ref[0, r0:r0 + ATTN_ROWS, :], False)

    lam = _diff_lambda(lq1, lk1, lq2, lk2, lam_init)
    o = acc_ref[0:tq, :] / l_ref[0:tq, :] - lam * (acc_ref[tq:2 * tq, :] / l_ref[tq:2 * tq, :])
    o_ref[...] = _subln(o, g_ref[...], lam_init).astype(BF16)


def _long_attn(q, k, v, k_meta, v_meta, bias_diag, bias_prev, bias_meta, lams, g, lam_init, seq_len):
    n, width = q.shape
    n_heads = width // HD2
    tq = ATTN_BLOCK
    n_seq, n_q = n // seq_len, seq_len // tq
    vec = lambda c: pl.BlockSpec((1, c), lambda b, h, i: (0, 0))
    return pl.pallas_call(
        functools.partial(_long_attn_kernel, lam_init=lam_init),
        grid=(n_seq, n_heads, n_q),
        in_specs=[pl.BlockSpec((tq, HD2), lambda b, h, i: (b * n_q + i, h)),
                  pl.BlockSpec((seq_len, HD2), lambda b, h, i: (b, h)),
                  pl.BlockSpec((seq_len, HD2), lambda b, h, i: (b, h)),
                  pl.BlockSpec((LANES, HD2), lambda b, h, i: (0, h)),
                  pl.BlockSpec((LANES, HD2), lambda b, h, i: (0, h)),
                  pl.BlockSpec((1, tq, tq), lambda b, h, i: (h, 0, 0)),
                  pl.BlockSpec((1, tq, tq), lambda b, h, i: (h, 0, 0)),
                  pl.BlockSpec((1, 1, tq, LANES), lambda b, h, i: (h, jnp.minimum(i, 1), 0, 0)),
                  vec(HEAD_DIM), vec(HEAD_DIM), vec(HEAD_DIM), vec(HEAD_DIM), vec(HD2)],
        out_specs=pl.BlockSpec((tq, HD2), lambda b, h, i: (b * n_q + i, h)),
        out_shape=jax.ShapeDtypeStruct(q.shape, BF16),
        scratch_shapes=[pltpu.VMEM((2 * tq, HD2), BF16),
                        pltpu.VMEM((2 * tq, 1), F32),
                        pltpu.VMEM((2 * tq, 1), F32),
                        pltpu.VMEM((2 * tq, HD2), F32)],
        compiler_params=pltpu.CompilerParams(dimension_semantics=("parallel", "parallel", "arbitrary")),
        name="long_attention",
    )(q, k, v, k_meta, v_meta, bias_diag, bias_prev, bias_meta, *lams, g)


def _pad_cols(idx, cols):
    return np.pad(idx, ((0, 0), (0, cols - idx.shape[1])), constant_values=MASKED)


def _long_bias_indices(n_meta):
    tq = ATTN_BLOCK
    r = np.arange(tq)[:, None]
    c = np.arange(tq)[None, :]
    diag = np.where(c // CHUNK <= r // CHUNK, _bucket_np(c - r), MASKED)
    prev = _bucket_np(c - tq - r)
    cm = np.arange(n_meta)[None, :]
    meta_first = _pad_cols(_bucket_np(cm - (n_meta + r)), LANES)
    meta_later = _pad_cols(_bucket_np(cm - (n_meta + tq + r)), LANES)
    return diag.astype(np.int32), prev.astype(np.int32), np.concatenate([meta_first, meta_later], 0).astype(np.int32)


def _short_bias_indices(q_pos, k_pos, cols):
    return _pad_cols(_bucket_np(k_pos[None, :] - q_pos[:, None]), cols).astype(np.int32)


def _round_up(x, m):
    return (x + m - 1) // m * m


def kernel(x_prompt, x_sample, cache_k, cache_v, state_pool, meta_tokens, rel_bias_table, norm_mix_pre,
           norm_mix_post, norm_ffn_pre, norm_ffn_post, w_qkv, lambda_q1, lambda_k1, lambda_q2, lambda_k2, subln_g,
           w_o, w_pool, pool_scale, w_up, w_down):
    bp, seq, d = x_prompt.shape
    bs, dec_seq, _ = x_sample.shape
    n_meta = meta_tokens.shape[0]
    past = cache_k.shape[2]
    depth = norm_mix_pre.shape[0]
    width = w_o.shape[1]
    assert seq % ATTN_BLOCK == 0 and dec_seq <= CHUNK and n_meta <= CHUNK and n_meta == HALO

    w_qkv_b, w_o_b, w_pool_b = w_qkv.astype(BF16), w_o.astype(BF16), w_pool.astype(BF16)
    w_up_b, w_down_b = w_up.astype(BF16), w_down.astype(BF16)
    vec = lambda a, i: a[i][None, :]

    diag_idx, prev_idx, meta_idx = _long_bias_indices(n_meta)
    bias_diag = _bias_tiles(rel_bias_table, diag_idx)
    bias_prev = _bias_tiles(rel_bias_table, prev_idx)
    bias_meta = _bias_tiles(rel_bias_table, meta_idx).reshape(N_HEADS, 2, ATTN_BLOCK, LANES)
    meta_pos = np.arange(n_meta)
    bias_self = _bias_tiles(rel_bias_table, _short_bias_indices(meta_pos, meta_pos, LANES))
    lk_s = _round_up(n_meta + past + dec_seq, LANES)
    k_pos_s = np.concatenate([meta_pos - n_meta, np.arange(past), past + np.arange(dec_seq)])
    bias_s = _bias_tiles(rel_bias_table, _short_bias_indices(past + np.arange(dec_seq), k_pos_s, lk_s))

    xm = meta_tokens
    xp = x_prompt.reshape(bp * seq, d)
    xs = x_sample.reshape(bs * dec_seq, d)
    meta_ks, meta_vs, kps, vps, kss, vss, tails_p, tails_s = [], [], [], [], [], [], [], []

    for i in range(depth):
        g_pre, g_post = vec(norm_mix_pre, i), vec(norm_mix_post, i)
        ffn = (vec(norm_ffn_pre, i), w_up_b[i], w_down_b[i], vec(norm_ffn_post, i))
        if i % 2 == 0:
            a = i // 2
            lam_init = 0.8 - 0.6 * math.exp(-0.3 * i)
            lams = (vec(lambda_q1, a), vec(lambda_k1, a), vec(lambda_q2, a), vec(lambda_k2, a))
            g_sub = vec(subln_g, a)
            qm, km, vm, kmb, vmb = _qkv(xm, g_pre, w_qkv_b[a])
            qp, kp, vp, kpb, vpb = _qkv(xp, g_pre, w_qkv_b[a])
            qs, ks, vs, ksb, vsb = _qkv(xs, g_pre, w_qkv_b[a])
            meta_ks.append(km), meta_vs.append(vm), kps.append(kp), vps.append(vp), kss.append(ks), vss.append(vs)
            kmb_pad = jnp.pad(kmb, ((0, LANES - n_meta), (0, 0)))
            vmb_pad = jnp.pad(vmb, ((0, LANES - n_meta), (0, 0)))
            om = _small_attn(qm, kmb_pad[None], vmb_pad[None], bias_self, lams, g_sub, lam_init)
            op = _long_attn(qp, kpb, vpb, kmb_pad, vmb_pad, bias_diag, bias_prev, bias_meta, lams, g_sub,
                            lam_init, seq)

            def sample_keys(meta_b, cache, new_b):
                rows = [jnp.broadcast_to(meta_b[None], (bs, n_meta, width)),
                        cache.reshape(bs, past, width).astype(BF16),
                        new_b.reshape(bs, dec_seq, width)]
                return jnp.pad(jnp.concatenate(rows, axis=1), ((0, 0), (0, lk_s - n_meta - past - dec_seq), (0, 0)))

            os_ = _small_attn(qs, sample_keys(kmb, cache_k[a], ksb), sample_keys(vmb, cache_v[a], vsb), bias_s,
                              lams, g_sub, lam_init)
            xm = _mlp(xm, *ffn, proj=(om, w_o_b[a], g_post))
            xp = _mlp(xp, *ffn, proj=(op, w_o_b[a], g_post))
            xs = _mlp(xs, *ffn, proj=(os_, w_o_b[a], g_post))
        else:
            p = i // 2
            pool_w = (g_pre, w_pool_b[p], vec(pool_scale, p), g_post)
            front = ((0, 0), (HALO - POOL_STATE, 0), (0, 0))
            xm1, tail_m = _pool(xm, jnp.zeros((1, HALO, d), F32), n_meta, 0, *pool_w)
            meta_tail = tail_m[:, HALO - POOL_STATE:]
            xp1, tail_p = _pool(xp, jnp.pad(meta_tail, front), seq, POOL_STATE, *pool_w)
            xs1, tail_s = _pool(xs, jnp.pad(state_pool[p], front), dec_seq, POOL_STATE, *pool_w)
            tails_p.append(jnp.broadcast_to(tail_p[:, HALO - POOL_STATE:], (bp, POOL_STATE, d)))
            tails_s.append(tail_s[:, HALO - POOL_STATE:])
            if i < depth - 1:
                xm = _mlp(xm1, *ffn)
            xp = _mlp(xp1, *ffn)
            xs = _mlp(xs1, *ffn)

    def with_meta(meta_rows, rows):
        lead = jnp.broadcast_to(meta_rows.reshape(1, n_meta, N_HEADS, HD2), (bp, n_meta, N_HEADS, HD2))
        return jnp.concatenate([lead, rows.reshape(bp, seq, N_HEADS, HD2)], axis=1)

    k_prompt = jnp.stack([with_meta(m, r) for m, r in zip(meta_ks, kps)])
    v_prompt = jnp.stack([with_meta(m, r) for m, r in zip(meta_vs, vps)])
    k_sample = jnp.stack([r.reshape(bs, dec_seq, N_HEADS, HD2) for r in kss])
    v_sample = jnp.stack([r.reshape(bs, dec_seq, N_HEADS, HD2) for r in vss])
    return (xp.reshape(bp, seq, d), xs.reshape(bs, dec_seq, d), k_prompt, v_prompt, jnp.stack(tails_p),
            k_sample, v_sample, jnp.stack(tails_s))
```

```python
import functools
import math

import jax
import jax.numpy as jnp
import numpy as np
from jax import lax
from jax.experimental import pallas as pl
from jax.experimental.pallas import tpu as pltpu

CHUNK = 64
N_HEADS = 8
HEAD_DIM = 64
HD2 = 2 * HEAD_DIM
N_BUCKETS = 32
MAX_DISTANCE = 128
POOL_WINDOWS = (2, 4, 8, 16)
POOL_STATE = max(POOL_WINDOWS) - 1
EPS = 1e-6
SUBLN_EPS = 1e-5
NEG_INF = -1e30
FAR_BUCKET = N_BUCKETS // 2 - 1
MASKED = N_BUCKETS

TOKEN_TILE = 512
ATTN_BLOCK = 512
ATTN_STEP = 2 * ATTN_BLOCK
ATTN_ROWS = 128
FF_CHUNK = 1024
HALO = 16
LANES = 128

BF16 = jnp.bfloat16
F32 = jnp.float32


def _rms(x, g, eps):
    return x * lax.rsqrt(jnp.mean(x * x, axis=-1, keepdims=True) + eps) * g


def _dot(a, b):
    return jnp.dot(a, b, preferred_element_type=F32)


def _dot_nt(a, b):
    return lax.dot_general(a, b, (((1,), (1,)), ((), ())), preferred_element_type=F32)


def _resident(shape):
    nd = len(shape)
    return pl.BlockSpec(shape, lambda *_: (0,) * nd, pipeline_mode=pl.Buffered(1))


def _qkv_kernel(x_ref, g_ref, w_ref, q_ref, k_ref, v_ref, kb_ref, vb_ref):
    width = q_ref.shape[1]
    h = _rms(x_ref[...], g_ref[...], EPS).astype(BF16)
    q = _dot(h, w_ref[:, 0:width])
    q_ref[...] = (q * (HEAD_DIM ** -0.5)).astype(BF16)
    k = _dot(h, w_ref[:, width:2 * width])
    k_ref[...] = k
    kb_ref[...] = k.astype(BF16)
    v = _dot(h, w_ref[:, 2 * width:3 * width])
    v_ref[...] = v
    vb_ref[...] = v.astype(BF16)


def _qkv(x, g, w):
    n, d = x.shape
    width = w.shape[1] // 3
    tm = min(TOKEN_TILE, n)
    row = lambda c: pl.BlockSpec((tm, c), lambda i: (i, 0))
    return pl.pallas_call(
        _qkv_kernel,
        grid=(n // tm,),
        in_specs=[row(d), _resident((1, d)), _resident(w.shape)],
        out_specs=[row(width)] * 5,
        out_shape=[jax.ShapeDtypeStruct((n, width), BF16),
                   jax.ShapeDtypeStruct((n, width), F32),
                   jax.ShapeDtypeStruct((n, width), F32),
                   jax.ShapeDtypeStruct((n, width), BF16),
                   jax.ShapeDtypeStruct((n, width), BF16)],
        compiler_params=pltpu.CompilerParams(dimension_semantics=("parallel",)),
        name="qkv_proj",
    )(x, g, w)


def _mlp_kernel(*refs, has_proj):
    if has_proj:
        x_ref, o_ref, wo_ref, gmix_ref, gpre_ref, wup_ref, wdn_ref, gpost_ref, out_ref = refs
    else:
        x_ref, gpre_ref, wup_ref, wdn_ref, gpost_ref, out_ref = refs
    x = x_ref[...]
    if has_proj:
        x = x + _rms(_dot(o_ref[...], wo_ref[...]), gmix_ref[...], EPS)
    h = _rms(x, gpre_ref[...], EPS).astype(BF16)
    d_ff = wup_ref.shape[1]
    acc = jnp.zeros(x.shape, F32)
    for c in range(d_ff // FF_CHUNK):
        cols = slice(c * FF_CHUNK, (c + 1) * FF_CHUNK)
        u = jnp.maximum(_dot(h, wup_ref[:, cols]), 0.0)
        acc = acc + _dot((u * u).astype(BF16), wdn_ref[cols, :])
    out_ref[...] = x + _rms(acc, gpost_ref[...], EPS)


def _mlp(x, g_pre, w_up, w_down, g_post, proj=None):
    n, d = x.shape
    tm = min(TOKEN_TILE, n)
    row = lambda c: pl.BlockSpec((tm, c), lambda i: (i, 0))
    args, specs = [x], [row(d)]
    if proj is not None:
        o, w_o, g_mix = proj
        args += [o, w_o, g_mix]
        specs += [row(o.shape[1]), _resident(w_o.shape), _resident((1, d))]
    args += [g_pre, w_up, w_down, g_post]
    specs += [_resident((1, d)), _resident(w_up.shape), _resident(w_down.shape), _resident((1, d))]
    return pl.pallas_call(
        functools.partial(_mlp_kernel, has_proj=proj is not None),
        grid=(n // tm,),
        in_specs=specs,
        out_specs=row(d),
        out_shape=jax.ShapeDtypeStruct((n, d), F32),
        compiler_params=pltpu.CompilerParams(dimension_semantics=("parallel",)),
        name="proj_mlp" if proj is not None else "mlp",
    )(*args)


def _pool_kernel(x_ref, xh_ref, prev_ref, gpre_ref, w_ref, scale_ref, gpost_ref, out_ref, tail_ref,
                 a_ref, b_ref, c_ref, d_ref, *, n_prev, tiles_per_seq):
    tm, d = x_ref.shape
    group = d // len(POOL_WINDOWS)
    pad = 8
    i = pl.program_id(0)
    t = i % tiles_per_seq
    x = x_ref[...]
    g = gpre_ref[...]
    h = _rms(x, g, EPS)
    halo = jnp.where(t == 0, prev_ref[0], _rms(xh_ref[...], g, EPS))
    for ref in (a_ref, b_ref, c_ref, d_ref):
        ref[0:pad, :] = jnp.zeros((pad, d), F32)
    a_ref[pad:pad + HALO, :] = halo
    a_ref[pad + HALO:pad + HALO + tm, :] = h
    rows = HALO + tm
    b_ref[pad:pad + rows, :] = a_ref[pad:pad + rows, :] + a_ref[pad - 1:pad - 1 + rows, :]
    c_ref[pad:pad + rows, group:] = b_ref[pad:pad + rows, group:] + b_ref[pad - 2:pad - 2 + rows, group:]
    d_ref[pad:pad + rows, 2 * group:] = (c_ref[pad:pad + rows, 2 * group:]
                                         + c_ref[pad - 4:pad - 4 + rows, 2 * group:])
    new = slice(pad + HALO, pad + HALO + tm)
    sums = (b_ref[new, 0:group],
            c_ref[new, group:2 * group],
            d_ref[new, 2 * group:3 * group],
            d_ref[new, 3 * group:] + d_ref[pad + HALO - 8:pad + HALO - 8 + tm, 3 * group:])
    cnt = (n_prev + 1 + t * tm + lax.broadcasted_iota(jnp.int32, (tm, 1), 0)).astype(F32)
    ys = []
    for gi, w in enumerate(POOL_WINDOWS):
        mean = sums[gi] / jnp.minimum(cnt, float(w))
        diff = (mean - h[:, gi * group:(gi + 1) * group]).astype(BF16)
        ys.append(_dot(diff, w_ref[gi]))
    y = jnp.concatenate(ys, axis=1) * scale_ref[...]
    out_ref[...] = x + _rms(y, gpost_ref[...], EPS)
    tail_ref[0] = h[tm - HALO:, :]


def _pool(x, prev, seq_len, n_prev, g_pre, w_pool, scale, g_post):
    n, d = x.shape
    tm = min(TOKEN_TILE, seq_len)
    tiles_per_seq = seq_len // tm
    n_seq = n // seq_len
    shared_prev = prev.shape[0] == 1
    halo_blocks = tm // HALO
    scratch = [pltpu.VMEM((8 + HALO + tm, d), F32)] * 4
    return pl.pallas_call(
        functools.partial(_pool_kernel, n_prev=n_prev, tiles_per_seq=tiles_per_seq),
        grid=(n // tm,),
        in_specs=[pl.BlockSpec((tm, d), lambda i: (i, 0)),
                  pl.BlockSpec((HALO, d), lambda i: (jnp.maximum(i * halo_blocks - 1, 0), 0)),
                  pl.BlockSpec((1, HALO, d), (lambda i: (0, 0, 0)) if shared_prev
                               else (lambda i: (i // tiles_per_seq, 0, 0))),
                  _resident((1, d)), _resident(w_pool.shape), _resident((1, d)), _resident((1, d))],
        out_specs=[pl.BlockSpec((tm, d), lambda i: (i, 0)),
                   pl.BlockSpec((1, HALO, d), lambda i: (i // tiles_per_seq, 0, 0))],
        out_shape=[jax.ShapeDtypeStruct((n, d), F32), jax.ShapeDtypeStruct((n_seq, HALO, d), F32)],
        scratch_shapes=scratch,
        compiler_params=pltpu.CompilerParams(dimension_semantics=("arbitrary",)),
        name="pool_mix",
    )(x, x, prev, g_pre, w_pool, scale, g_post)


def _bucket_np(rel):
    nb = N_BUCKETS // 2
    max_exact = nb // 2
    n = np.abs(rel)
    ratio = np.log(np.maximum(n, 1) / max_exact) / math.log(MAX_DISTANCE / max_exact) * (nb - max_exact)
    large = np.minimum(max_exact + np.floor(ratio + 1e-6).astype(np.int64), nb - 1)
    return (np.where(rel > 0, nb, 0) + np.where(n < max_exact, n, large)).astype(np.int32)


def _bias_kernel(tab_ref, idx_ref, out_ref):
    h = pl.program_id(0)
    idx = idx_ref[...]
    base = tab_ref[FAR_BUCKET, h]
    out = jnp.full(idx.shape, NEG_INF, F32)
    for b in range(N_BUCKETS):
        out = jnp.where(idx == b, tab_ref[b, h] - base, out)
    out_ref[0] = out


def _bias_tiles(table, idx):
    r, c = idx.shape
    tr = min(r, 256)
    n_heads = table.shape[1]
    return pl.pallas_call(
        _bias_kernel,
        grid=(n_heads, r // tr),
        in_specs=[pl.BlockSpec(memory_space=pltpu.SMEM), pl.BlockSpec((tr, c), lambda h, i: (i, 0))],
        out_specs=pl.BlockSpec((1, tr, c), lambda h, i: (h, i, 0)),
        out_shape=jax.ShapeDtypeStruct((n_heads, r, c), F32),
        compiler_params=pltpu.CompilerParams(dimension_semantics=("parallel", "parallel")),
        name="rel_bias_tiles",
    )(table, jnp.asarray(idx))


def _diff_lambda(lq1, lk1, lq2, lk2, lam_init):
    return (jnp.exp(jnp.sum(lq1[...] * lk1[...], keepdims=True))
            - jnp.exp(jnp.sum(lq2[...] * lk2[...], keepdims=True)) + lam_init)


def _split_maps(q):
    lane = lax.broadcasted_iota(jnp.int32, q.shape, 1)
    zero = jnp.zeros_like(q)
    return jnp.where(lane < HEAD_DIM, q, zero), jnp.where(lane >= HEAD_DIM, q, zero)


def _subln(o, g, lam_init):
    return _rms(o, g, SUBLN_EPS) * (1.0 - lam_init)


def _small_attn_kernel(q_ref, k_ref, v_ref, bias_ref, lq1, lk1, lq2, lk2, g_ref, o_ref, *, lam_init):
    q1, q2 = _split_maps(q_ref[...])
    k = k_ref[0]
    bias = bias_ref[0]

    def softmax(s):
        e = jnp.exp(s - s.max(axis=1, keepdims=True))
        return e / e.sum(axis=1, keepdims=True)

    lam = _diff_lambda(lq1, lk1, lq2, lk2, lam_init)
    p = softmax(_dot_nt(q1, k) + bias) - lam * softmax(_dot_nt(q2, k) + bias)
    o = _dot(p.astype(BF16), v_ref[0])
    o_ref[...] = _subln(o, g_ref[...], lam_init).astype(BF16)


def _small_attn(q, k_all, v_all, bias, lams, g, lam_init):
    n_seq, lk, width = k_all.shape
    lq = q.shape[0] // n_seq
    n_heads = width // HD2
    vec = lambda c: pl.BlockSpec((1, c), lambda b, h: (0, 0))
    return pl.pallas_call(
        functools.partial(_small_attn_kernel, lam_init=lam_init),
        grid=(n_seq, n_heads),
        in_specs=[pl.BlockSpec((lq, HD2), lambda b, h: (b, h)),
                  pl.BlockSpec((1, lk, HD2), lambda b, h: (b, 0, h)),
                  pl.BlockSpec((1, lk, HD2), lambda b, h: (b, 0, h)),
                  pl.BlockSpec((1, lq, lk), lambda b, h: (h, 0, 0)),
                  vec(HEAD_DIM), vec(HEAD_DIM), vec(HEAD_DIM), vec(HEAD_DIM), vec(HD2)],
        out_specs=pl.BlockSpec((lq, HD2), lambda b, h: (b, h)),
        out_shape=jax.ShapeDtypeStruct(q.shape, BF16),
        compiler_params=pltpu.CompilerParams(dimension_semantics=("parallel", "parallel")),
        name="short_attention",
    )(q, k_all, v_all, bias, *lams, g)


def _long_attn_kernel(q_ref, k_ref, v_ref, km_ref, vm_ref, bd_ref, bz_ref, bm_ref, lq1, lk1, lq2, lk2, g_ref,
                      o_ref, qs_ref, m_ref, acc_ref, *, lam_init):
    tq = q_ref.shape[0]
    kb = ATTN_BLOCK
    step = pl.program_id(2)
    q1, q2 = _split_maps(q_ref[...])
    qs_ref[0:tq, :] = q1
    qs_ref[tq:2 * tq, :] = q2
    all_chunks = range(2 * tq // ATTN_ROWS)
    block_of = lambda c: (c * ATTN_ROWS) % tq // kb
    rows_of = lambda ref, c: ref[0, pl.ds((c * ATTN_ROWS) % kb, ATTN_ROWS), :]

    def update(kblk, vblk, bias_of, chunks=all_chunks):
        vx = jnp.concatenate([vblk, jnp.ones_like(vblk)], axis=1)
        for c in chunks:
            rows = slice(c * ATTN_ROWS, (c + 1) * ATTN_ROWS)
            s = _dot_nt(qs_ref[rows, :], kblk)
            bias = bias_of(c)
            if bias is not None:
                s = s + bias
            slabs = [s[:, t * LANES:(t + 1) * LANES] for t in range(s.shape[1] // LANES)]
            m_prev = m_ref[rows, :]
            m_new = jnp.maximum(m_prev, jnp.max(functools.reduce(jnp.maximum, slabs), axis=1, keepdims=True))
            p = jnp.concatenate([jnp.exp(sl - m_new) for sl in slabs], axis=1).astype(BF16)
            alpha = jnp.exp(m_prev - m_new)
            acc_ref[rows, :] = jnp.concatenate([alpha, alpha], axis=1) * acc_ref[rows, :] + _dot(p, vx)
            m_ref[rows, :] = m_new

    def key_block(j):
        off = pl.multiple_of(j * kb, kb)
        return k_ref[pl.ds(off, kb), :], v_ref[pl.ds(off, kb), :]

    m_ref[...] = jnp.full(m_ref.shape, NEG_INF, F32)
    acc_ref[...] = jnp.zeros(acc_ref.shape, F32)

    def pair_body(jj, carry):
        update(*key_block(2 * jj), lambda c: None)
        near = (jj == step - 1).astype(jnp.int32)
        update(*key_block(2 * jj + 1),
               lambda c: bz_ref[0, near, pl.ds((c * ATTN_ROWS) % kb, ATTN_ROWS), :] if block_of(c) == 0 else None)
        return carry

    lax.fori_loop(0, step, pair_body, 0)

    k_own, v_own = key_block(2 * step)
    meta_rows = lambda c: bm_ref[0, 0, pl.ds((c * ATTN_ROWS) % tq, ATTN_ROWS), :]
    prev_rows = lambda c: bz_ref[0, 1, pl.ds((c * ATTN_ROWS) % kb, ATTN_ROWS), :]
    update(jnp.concatenate([k_own, km_ref[...]], axis=0), jnp.concatenate([v_own, vm_ref[...]], axis=0),
           lambda c: jnp.concatenate([rows_of(bd_ref, c) if block_of(c) == 0 else prev_rows(c), meta_rows(c)], axis=1))
    update(*key_block(2 * step + 1), lambda c: rows_of(bd_ref, c), chunks=[c for c in all_chunks if block_of(c) == 1])

    lam = _diff_lambda(lq1, lk1, lq2, lk2, lam_init)
    o = (acc_ref[0:tq, 0:HD2] / acc_ref[0:tq, HD2:]
         - lam * (acc_ref[tq:2 * tq, 0:HD2] / acc_ref[tq:2 * tq, HD2:]))
    o_ref[...] = _subln(o, g_ref[...], lam_init).astype(BF16)


def _long_attn(q, k, v, k_meta, v_meta, bias_diag, bias_zero_prev, bias_meta, lams, g, lam_init, seq_len):
    n, width = q.shape
    n_heads = width // HD2
    tq, kb = ATTN_STEP, ATTN_BLOCK
    n_seq, n_q = n // seq_len, seq_len // tq
    vec = lambda c: pl.BlockSpec((1, c), lambda b, h, i: (0, 0))
    return pl.pallas_call(
        functools.partial(_long_attn_kernel, lam_init=lam_init),
        grid=(n_seq, n_heads, n_q),
        in_specs=[pl.BlockSpec((tq, HD2), lambda b, h, i: (b * n_q + i, h)),
                  pl.BlockSpec((seq_len, HD2), lambda b, h, i: (b, h)),
                  pl.BlockSpec((seq_len, HD2), lambda b, h, i: (b, h)),
                  pl.BlockSpec((LANES, HD2), lambda b, h, i: (0, h)),
                  pl.BlockSpec((LANES, HD2), lambda b, h, i: (0, h)),
                  pl.BlockSpec((1, kb, kb), lambda b, h, i: (h, 0, 0)),
                  pl.BlockSpec((1, 2, kb, kb), lambda b, h, i: (h, 0, 0, 0)),
                  pl.BlockSpec((1, 1, tq, LANES), lambda b, h, i: (h, jnp.minimum(i, 1), 0, 0)),
                  vec(HEAD_DIM), vec(HEAD_DIM), vec(HEAD_DIM), vec(HEAD_DIM), vec(HD2)],
        out_specs=pl.BlockSpec((tq, HD2), lambda b, h, i: (b * n_q + i, h)),
        out_shape=jax.ShapeDtypeStruct(q.shape, BF16),
        scratch_shapes=[pltpu.VMEM((2 * tq, HD2), BF16),
                        pltpu.VMEM((2 * tq, LANES), F32),
                        pltpu.VMEM((2 * tq, 2 * HD2), F32)],
        compiler_params=pltpu.CompilerParams(dimension_semantics=("parallel", "parallel", "arbitrary")),
        name="long_attention",
    )(q, k, v, k_meta, v_meta, bias_diag, bias_zero_prev, bias_meta, *lams, g)


def _pad_cols(idx, cols):
    return np.pad(idx, ((0, 0), (0, cols - idx.shape[1])), constant_values=MASKED)


def _long_bias_indices(n_meta):
    kb = ATTN_BLOCK
    r = np.arange(kb)[:, None]
    c = np.arange(kb)[None, :]
    diag = np.where(c // CHUNK <= r // CHUNK, _bucket_np(c - r), MASKED)
    prev = _bucket_np(c - kb - r)
    rq = np.arange(ATTN_STEP)[:, None]
    cm = np.arange(n_meta)[None, :]
    meta_first = _pad_cols(_bucket_np(cm - (n_meta + rq)), LANES)
    meta_later = _pad_cols(_bucket_np(cm - (n_meta + ATTN_STEP + rq)), LANES)
    return diag.astype(np.int32), prev.astype(np.int32), np.concatenate([meta_first, meta_later], 0).astype(np.int32)


def _short_bias_indices(q_pos, k_pos, cols):
    return _pad_cols(_bucket_np(k_pos[None, :] - q_pos[:, None]), cols).astype(np.int32)


def _round_up(x, m):
    return (x + m - 1) // m * m


def kernel(x_prompt, x_sample, cache_k, cache_v, state_pool, meta_tokens, rel_bias_table, norm_mix_pre,
           norm_mix_post, norm_ffn_pre, norm_ffn_post, w_qkv, lambda_q1, lambda_k1, lambda_q2, lambda_k2, subln_g,
           w_o, w_pool, pool_scale, w_up, w_down):
    bp, seq, d = x_prompt.shape
    bs, dec_seq, _ = x_sample.shape
    n_meta = meta_tokens.shape[0]
    past = cache_k.shape[2]
    depth = norm_mix_pre.shape[0]
    width = w_o.shape[1]
    assert seq % ATTN_STEP == 0 and dec_seq <= CHUNK and n_meta <= CHUNK and n_meta == HALO

    w_qkv_b, w_o_b, w_pool_b = w_qkv.astype(BF16), w_o.astype(BF16), w_pool.astype(BF16)
    w_up_b, w_down_b = w_up.astype(BF16), w_down.astype(BF16)
    vec = lambda a, i: a[i][None, :]

    diag_idx, prev_idx, meta_idx = _long_bias_indices(n_meta)
    bias_diag = _bias_tiles(rel_bias_table, diag_idx)
    zero_prev_idx = np.concatenate([np.full_like(prev_idx, FAR_BUCKET), prev_idx], 0)
    bias_zero_prev = _bias_tiles(rel_bias_table, zero_prev_idx).reshape(N_HEADS, 2, ATTN_BLOCK, ATTN_BLOCK)
    bias_meta = _bias_tiles(rel_bias_table, meta_idx).reshape(N_HEADS, 2, ATTN_STEP, LANES)
    meta_pos = np.arange(n_meta)
    bias_self = _bias_tiles(rel_bias_table, _short_bias_indices(meta_pos, meta_pos, LANES))
    lk_s = _round_up(n_meta + past + dec_seq, LANES)
    k_pos_s = np.concatenate([meta_pos - n_meta, np.arange(past), past + np.arange(dec_seq)])
    bias_s = _bias_tiles(rel_bias_table, _short_bias_indices(past + np.arange(dec_seq), k_pos_s, lk_s))

    xm = meta_tokens
    xp = x_prompt.reshape(bp * seq, d)
    xs = x_sample.reshape(bs * dec_seq, d)
    meta_ks, meta_vs, kps, vps, kss, vss, tails_p, tails_s = [], [], [], [], [], [], [], []

    for i in range(depth):
        g_pre, g_post = vec(norm_mix_pre, i), vec(norm_mix_post, i)
        ffn = (vec(norm_ffn_pre, i), w_up_b[i], w_down_b[i], vec(norm_ffn_post, i))
        if i % 2 == 0:
            a = i // 2
            lam_init = 0.8 - 0.6 * math.exp(-0.3 * i)
            lams = (vec(lambda_q1, a), vec(lambda_k1, a), vec(lambda_q2, a), vec(lambda_k2, a))
            g_sub = vec(subln_g, a)
            qm, km, vm, kmb, vmb = _qkv(xm, g_pre, w_qkv_b[a])
            qp, kp, vp, kpb, vpb = _qkv(xp, g_pre, w_qkv_b[a])
            qs, ks, vs, ksb, vsb = _qkv(xs, g_pre, w_qkv_b[a])
            meta_ks.append(km), meta_vs.append(vm), kps.append(kp), vps.append(vp), kss.append(ks), vss.append(vs)
            kmb_pad = jnp.pad(kmb, ((0, LANES - n_meta), (0, 0)))
            vmb_pad = jnp.pad(vmb, ((0, LANES - n_meta), (0, 0)))
            om = _small_attn(qm, kmb_pad[None], vmb_pad[None], bias_self, lams, g_sub, lam_init)
            op = _long_attn(qp, kpb, vpb, kmb_pad, vmb_pad, bias_diag, bias_zero_prev, bias_meta, lams, g_sub,
                            lam_init, seq)

            def sample_keys(meta_b, cache, new_b):
                rows = [jnp.broadcast_to(meta_b[None], (bs, n_meta, width)),
                        cache.reshape(bs, past, width).astype(BF16),
                        new_b.reshape(bs, dec_seq, width)]
                return jnp.pad(jnp.concatenate(rows, axis=1), ((0, 0), (0, lk_s - n_meta - past - dec_seq), (0, 0)))

            os_ = _small_attn(qs, sample_keys(kmb, cache_k[a], ksb), sample_keys(vmb, cache_v[a], vsb), bias_s,
                              lams, g_sub, lam_init)
            xm = _mlp(xm, *ffn, proj=(om, w_o_b[a], g_post))
            xp = _mlp(xp, *ffn, proj=(op, w_o_b[a], g_post))
            xs = _mlp(xs, *ffn, proj=(os_, w_o_b[a], g_post))
        else:
            p = i // 2
            pool_w = (g_pre, w_pool_b[p], vec(pool_scale, p), g_post)
            front = ((0, 0), (HALO - POOL_STATE, 0), (0, 0))
            xm1, tail_m = _pool(xm, jnp.zeros((1, HALO, d), F32), n_meta, 0, *pool_w)
            meta_tail = tail_m[:, HALO - POOL_STATE:]
            xp1, tail_p = _pool(xp, jnp.pad(meta_tail, front), seq, POOL_STATE, *pool_w)
            xs1, tail_s = _pool(xs, jnp.pad(state_pool[p], front), dec_seq, POOL_STATE, *pool_w)
            tails_p.append(tail_p[:, HALO - POOL_STATE:])
            tails_s.append(tail_s[:, HALO - POOL_STATE:])
            if i < depth - 1:
                xm = _mlp(xm1, *ffn)
            xp = _mlp(xp1, *ffn)
            xs = _mlp(xs1, *ffn)

    def with_meta(meta_rows, rows):
        lead = jnp.broadcast_to(meta_rows.reshape(1, n_meta, N_HEADS, HD2), (bp, n_meta, N_HEADS, HD2))
        return jnp.concatenate([lead, rows.reshape(bp, seq, N_HEADS, HD2)], axis=1)

    k_prompt = jnp.stack([with_meta(m, r) for m, r in zip(meta_ks, kps)])
    v_prompt = jnp.stack([with_meta(m, r) for m, r in zip(meta_vs, vps)])
    k_sample = jnp.stack([r.reshape(bs, dec_seq, N_HEADS, HD2) for r in kss])
    v_sample = jnp.stack([r.reshape(bs, dec_seq, N_HEADS, HD2) for r in vss])
    return (xp.reshape(bp, seq, d), xs.reshape(bs, dec_seq, d), k_prompt, v_prompt, jnp.stack(tails_p),
            k_sample, v_sample, jnp.stack(tails_s))
```
